```python
import math
import jax, jax.numpy as jnp
from jax import lax
import numpy as np

D_MODEL = 1024
BATCH = 2
SEQ = 8192
DEPTH = 1
DEC_BATCH = 128
DEC_SEQ = 4
PAST_LEN = 8192
PAGE_SIZE = 128

D_ATT = D_MODEL // 2
D_SSM = D_MODEL - D_ATT
D_MIX = D_ATT + D_SSM
HEAD_DIM = 64
N_ATT_HEADS = D_ATT // HEAD_DIM
SSM_GROUP = 16
N_SSM_GROUPS = D_SSM // SSM_GROUP
SSM_STATE = 64
PLE_DIM = 256
Q_BLOCK = 128
D_PROJ = 3 * D_ATT + N_ATT_HEADS + D_ATT + 2 * D_SSM
RMS_EPS = 1e-6
FORGET_BIAS_INIT = 5.0
POOL_NUM, POOL_DEN = 5, 4

kernel_name = "fox_s5_parallel_hybrid_step"


def rms_norm(x, g):
    xf = x.astype(jnp.float32)
    y = xf * lax.rsqrt(jnp.mean(xf * xf, axis=-1, keepdims=True) + RMS_EPS)
    return (y * g.astype(jnp.float32)).astype(x.dtype)


def in_projection(x, norm_in, w_in, b_forget, q_norm, k_norm):
    b, s, _ = x.shape
    h = rms_norm(x, norm_in)
    proj = h @ w_in
    cuts = [D_ATT, 2 * D_ATT, 3 * D_ATT, 3 * D_ATT + N_ATT_HEADS,
            4 * D_ATT + N_ATT_HEADS, 4 * D_ATT + N_ATT_HEADS + D_SSM]
    q, k, v, f_logit, z_att, u, z_ssm = jnp.split(proj, cuts, axis=-1)
    q = rms_norm(q.reshape(b, s, N_ATT_HEADS, HEAD_DIM), q_norm)
    k = rms_norm(k.reshape(b, s, N_ATT_HEADS, HEAD_DIM), k_norm)
    v = v.reshape(b, s, N_ATT_HEADS, HEAD_DIM)
    log_f = jax.nn.log_sigmoid((f_logit + b_forget).astype(jnp.float32))
    return q, k, v, log_f, z_att, u, z_ssm


def fox_logits(q, k, c_q, c_k):
    s = jnp.einsum('bqhd,bkhd->bhqk', q, k, preferred_element_type=jnp.float32)
    bias = c_q.transpose(0, 2, 1)[:, :, :, None] - c_k.transpose(0, 2, 1)[:, :, None, :]
    return s * (HEAD_DIM ** -0.5) + bias


def prompt_attention(q, k, v, log_f):
    b, s = q.shape[0], q.shape[1]
    c = lax.cumsum(log_f, axis=1)
    k_pos = jnp.arange(s)

    def block(i):
        start = i * Q_BLOCK
        qb = lax.dynamic_slice_in_dim(q, start, Q_BLOCK, axis=1)
        cb = lax.dynamic_slice_in_dim(c, start, Q_BLOCK, axis=1)
        q_pos = start + jnp.arange(Q_BLOCK)
        logits = fox_logits(qb, k, cb, c)
        logits = jnp.where(k_pos[None, None, None, :] <= q_pos[None, None, :, None], logits, -jnp.inf)
        p = jax.nn.softmax(logits, axis=-1)
        return jnp.einsum('bhqk,bkhd->bqhd', p.astype(v.dtype), v)

    out = lax.map(block, jnp.arange(s // Q_BLOCK))
    return jnp.moveaxis(out, 0, 1).reshape(b, s, D_ATT)


def sample_attention(q, k, v, log_f, k_past, v_past, lf_past):
    b, n = q.shape[0], q.shape[1]
    past_len = k_past.shape[1]
    c_past = lax.cumsum(lf_past.astype(jnp.float32), axis=1)
    c_new = c_past[:, -1:] + lax.cumsum(log_f, axis=1)
    logits_past = fox_logits(q, k_past, c_new, c_past)
    causal = jnp.arange(n)[None, :] <= jnp.arange(n)[:, None]
    logits_new = jnp.where(causal, fox_logits(q, k, c_new, c_new), -jnp.inf)
    p = jax.nn.softmax(jnp.concatenate([logits_past, logits_new], axis=-1), axis=-1).astype(v.dtype)
    out = (jnp.einsum('bhqk,bkhd->bqhd', p[..., :past_len], v_past)
           + jnp.einsum('bhqk,bkhd->bqhd', p[..., past_len:], v))
    return out.reshape(b, n, D_ATT)


def ssm_branch(u, h0_re, h0_im, a_re, a_im, log_dt, b_re, b_im, c_re, c_im, d, w_glu, b_glu):
    bsz, s, _ = u.shape
    uf = u.astype(jnp.float32).reshape(bsz, s, N_SSM_GROUPS, SSM_GROUP)
    lam = lax.complex(a_re.astype(jnp.float32), a_im.astype(jnp.float32))
    dt = jnp.exp(log_dt.astype(jnp.float32))[:, None]
    lam_bar = jnp.exp(lam * dt)
    b_bar = ((lam_bar - 1.0) / lam)[..., None] * lax.complex(
        b_re.astype(jnp.float32), b_im.astype(jnp.float32))
    c_mat = lax.complex(c_re.astype(jnp.float32), c_im.astype(jnp.float32))
    bu = jnp.einsum('bsgc,gpc->bsgp', uf.astype(jnp.complex64), b_bar)
    h0 = lax.complex(h0_re.astype(jnp.float32), h0_im.astype(jnp.float32))
    bu = bu.at[:, 0].add(lam_bar * h0)
    a = jnp.broadcast_to(lam_bar, bu.shape)

    def combine(l, r):
        return (l[0] * r[0], r[0] * l[1] + r[1])

    _, h = lax.associative_scan(combine, (a, bu), axis=1)
    y = jnp.real(jnp.einsum('bsgp,gcp->bsgc', h, c_mat)) + d.astype(jnp.float32) * uf
    y = y.reshape(bsz, s, D_SSM).astype(u.dtype)
    g = jax.nn.gelu(y)
    y = g * jax.nn.sigmoid(g @ w_glu + b_glu)
    h_last = h[:, -1]
    return y, jnp.real(h_last), jnp.imag(h_last)


def merge(x, o_att, z_att, y_ssm, z_ssm, p, att_out_norm, ssm_out_norm, w_out, w_ple, w_ple_gate):
    o_att = rms_norm(o_att * jax.nn.silu(z_att), att_out_norm)
    y_ssm = rms_norm(y_ssm * jax.nn.silu(z_ssm), ssm_out_norm)
    h = x + jnp.concatenate([o_att, y_ssm], axis=-1) @ w_out
    return h + jax.nn.sigmoid(h @ w_ple_gate) * (p @ w_ple)


def setup_inputs(seed: int = 0) -> dict:
    key = jax.random.key(seed)
    ks = jax.random.split(key, 40)
    f32 = jnp.float32
    nrm = lambda k, shape, scale=1.0: scale * jax.random.normal(k, shape, f32)
    n_pages = PAST_LEN // PAGE_SIZE
    n_used = DEC_BATCH * n_pages
    n_pool = (n_used * POOL_NUM) // POOL_DEN
    page_table = jax.random.permutation(ks[0], n_pool)[:n_used].reshape(DEC_BATCH, n_pages).astype(jnp.int32)
    log_dt = jax.random.uniform(ks[20], (DEPTH, N_SSM_GROUPS), f32, math.log(0.001), math.log(0.1))
    a_im = jnp.pi * jnp.arange(SSM_STATE, dtype=f32)[None, None, :] + nrm(ks[21], (DEPTH, N_SSM_GROUPS, SSM_STATE), 0.01)
    return {
        "x_prompt": nrm(ks[1], (BATCH, SEQ, D_MODEL)),
        "x_sample": nrm(ks[2], (DEC_BATCH, DEC_SEQ, D_MODEL)),
        "p_prompt": nrm(ks[3], (DEPTH, BATCH, SEQ, PLE_DIM)),
        "p_sample": nrm(ks[4], (DEPTH, DEC_BATCH, DEC_SEQ, PLE_DIM)),
        "cache_k": nrm(ks[5], (DEPTH, n_pool, PAGE_SIZE, N_ATT_HEADS, HEAD_DIM)),
        "cache_v": nrm(ks[6], (DEPTH, n_pool, PAGE_SIZE, N_ATT_HEADS, HEAD_DIM)),
        "cache_logf": jax.nn.log_sigmoid(FORGET_BIAS_INIT + nrm(ks[7], (DEPTH, n_pool, PAGE_SIZE, N_ATT_HEADS))),
        "state_ssm_re": nrm(ks[8], (DEPTH, DEC_BATCH, N_SSM_GROUPS, SSM_STATE), 0.1),
        "state_ssm_im": nrm(ks[9], (DEPTH, DEC_BATCH, N_SSM_GROUPS, SSM_STATE), 0.1),
        "page_table": page_table,
        "norm_in": 1.0 + nrm(ks[10], (DEPTH, D_MODEL), 0.01),
        "w_in": nrm(ks[11], (DEPTH, D_MODEL, D_PROJ), D_MODEL ** -0.5),
        "b_forget": FORGET_BIAS_INIT + nrm(ks[12], (DEPTH, N_ATT_HEADS), 0.5),
        "q_norm": 1.0 + nrm(ks[13], (DEPTH, HEAD_DIM), 0.01),
        "k_norm": 1.0 + nrm(ks[14], (DEPTH, HEAD_DIM), 0.01),
        "ssm_a_re": -0.5 + nrm(ks[15], (DEPTH, N_SSM_GROUPS, SSM_STATE), 0.01),
        "ssm_a_im": a_im,
        "ssm_log_dt": log_dt,
        "ssm_b_re": nrm(ks[16], (DEPTH, N_SSM_GROUPS, SSM_STATE, SSM_GROUP), (2.0 * SSM_GROUP) ** -0.5),
        "ssm_b_im": nrm(ks[17], (DEPTH, N_SSM_GROUPS, SSM_STATE, SSM_GROUP), (2.0 * SSM_GROUP) ** -0.5),
        "ssm_c_re": nrm(ks[18], (DEPTH, N_SSM_GROUPS, SSM_GROUP, SSM_STATE), (2.0 * SSM_STATE) ** -0.5),
        "ssm_c_im": nrm(ks[19], (DEPTH, N_SSM_GROUPS, SSM_GROUP, SSM_STATE), (2.0 * SSM_STATE) ** -0.5),
        "ssm_d": nrm(ks[22], (DEPTH, N_SSM_GROUPS, SSM_GROUP)),
        "w_glu": nrm(ks[23], (DEPTH, D_SSM, D_SSM), D_SSM ** -0.5),
        "b_glu": nrm(ks[24], (DEPTH, D_SSM), 0.01),
        "att_out_norm": 1.0 + nrm(ks[25], (DEPTH, D_ATT), 0.01),
        "ssm_out_norm": 1.0 + nrm(ks[26], (DEPTH, D_SSM), 0.01),
        "w_out": nrm(ks[27], (DEPTH, D_MIX, D_MODEL), D_MIX ** -0.5),
        "w_ple": nrm(ks[28], (DEPTH, PLE_DIM, D_MODEL), PLE_DIM ** -0.5),
        "w_ple_gate": nrm(ks[29], (DEPTH, D_MODEL, D_MODEL), D_MODEL ** -0.5),
    }


def reference(x_prompt, x_sample, p_prompt, p_sample, cache_k, cache_v, cache_logf,
              state_ssm_re, state_ssm_im, page_table, norm_in, w_in, b_forget, q_norm, k_norm,
              ssm_a_re, ssm_a_im, ssm_log_dt, ssm_b_re, ssm_b_im, ssm_c_re, ssm_c_im, ssm_d,
              w_glu, b_glu, att_out_norm, ssm_out_norm, w_out, w_ple, w_ple_gate):
    xp, xs = x_prompt, x_sample
    bp, bs = x_prompt.shape[0], x_sample.shape[0]
    kp_l, vp_l, fp_l, ks_l, vs_l, fs_l = [], [], [], [], [], []
    srp_l, sip_l, srs_l, sis_l = [], [], [], []
    for i in range(DEPTH):
        ssm_w = (ssm_a_re[i], ssm_a_im[i], ssm_log_dt[i], ssm_b_re[i], ssm_b_im[i],
                 ssm_c_re[i], ssm_c_im[i], ssm_d[i], w_glu[i], b_glu[i])
        merge_w = (att_out_norm[i], ssm_out_norm[i], w_out[i], w_ple[i], w_ple_gate[i])

        q, k, v, lf, za, u, zs = in_projection(xp, norm_in[i], w_in[i], b_forget[i], q_norm[i], k_norm[i])
        o_att = prompt_attention(q, k, v, lf)
        h0 = jnp.zeros((bp, N_SSM_GROUPS, SSM_STATE), jnp.float32)
        y_ssm, h_re, h_im = ssm_branch(u, h0, h0, *ssm_w)
        kp_l.append(k); vp_l.append(v); fp_l.append(lf)
        srp_l.append(h_re); sip_l.append(h_im)
        xp = merge(xp, o_att, za, y_ssm, zs, p_prompt[i], *merge_w)

        q, k, v, lf, za, u, zs = in_projection(xs, norm_in[i], w_in[i], b_forget[i], q_norm[i], k_norm[i])
        k_past = cache_k[i, page_table].reshape(bs, -1, N_ATT_HEADS, HEAD_DIM)
        v_past = cache_v[i, page_table].reshape(bs, -1, N_ATT_HEADS, HEAD_DIM)
        lf_past = cache_logf[i, page_table].reshape(bs, -1, N_ATT_HEADS)
        o_att = sample_attention(q, k, v, lf, k_past, v_past, lf_past)
        y_ssm, h_re, h_im = ssm_branch(u, state_ssm_re[i], state_ssm_im[i], *ssm_w)
        ks_l.append(k); vs_l.append(v); fs_l.append(lf)
        srs_l.append(h_re); sis_l.append(h_im)
        xs = merge(xs, o_att, za, y_ssm, zs, p_sample[i], *merge_w)

    k_prompt, v_prompt, logf_prompt = jnp.stack(kp_l), jnp.stack(vp_l), jnp.stack(fp_l)
    k_sample, v_sample, logf_sample = jnp.stack(ks_l), jnp.stack(vs_l), jnp.stack(fs_l)
    ssm_re_prompt, ssm_im_prompt = jnp.stack(srp_l), jnp.stack(sip_l)
    ssm_re_sample, ssm_im_sample = jnp.stack(srs_l), jnp.stack(sis_l)
    return (xp, xs, k_prompt, v_prompt, logf_prompt, k_sample, v_sample, logf_sample,
            ssm_re_prompt, ssm_im_prompt, ssm_re_sample, ssm_im_sample)
```

```python
import functools
import math

import jax
import jax.numpy as jnp
from jax import lax
from jax.experimental import pallas as pl
from jax.experimental.pallas import tpu as pltpu

F32 = jnp.float32
BF16 = jnp.bfloat16

HEAD_DIM = 64
N_HEADS = 8
D_ATT = N_HEADS * HEAD_DIM
SSM_GROUP = 16
N_GROUPS = 32
D_SSM = N_GROUPS * SSM_GROUP
SSM_STATE = 64
HALF_STATE = N_GROUPS * SSM_STATE
STATE_W = 2 * HALF_STATE
RMS_EPS = 1e-6
LANES = 128
SUBLANES = 8
DEC_ROWS = SUBLANES
VMEM_LIMIT = 56 * 1024 * 1024

AUG_C = HEAD_DIM
AUG_ONE = HEAD_DIM + 3


def _split3(x):
    hi = x.astype(BF16)
    r1 = x - hi.astype(F32)
    mid = r1.astype(BF16)
    lo = (r1 - mid.astype(F32)).astype(BF16)
    return hi, mid, lo


def _dot(a, b):
    return jnp.dot(a, b, preferred_element_type=F32)


def _dot_nt(a, b):
    return lax.dot_general(a, b, (((1,), (1,)), ((), ())), preferred_element_type=F32)


def _dot3(a_f32, w_bf16):
    hi, mid, lo = _split3(a_f32)
    return _dot(hi, w_bf16) + _dot(mid, w_bf16) + _dot(lo, w_bf16)


def _sigmoid(x):
    return 1.0 / (1.0 + jnp.exp(-x))


def _silu(x):
    return x * _sigmoid(x)


def _gelu_tanh(x):
    return 0.5 * x * (1.0 + jnp.tanh(math.sqrt(2.0 / math.pi) * (x + 0.044715 * (x * x * x))))


def _rms(x, g):
    return x * lax.rsqrt(jnp.mean(x * x, axis=-1, keepdims=True) + RMS_EPS) * g


def _inproj_body(x_ref, nin_ref, w_ref, wf_ref, bf_ref, qg_ref, kg_ref, bones_ref, tri_ref,
                 *rest, prompt):
    if prompt:
        (k_ref, v_ref, lf_ref, za_ref, u_ref, zs_ref, qa_ref, ka_ref, vta_ref, carry_ref) = rest
    else:
        (k_ref, v_ref, lf_ref, za_ref, u_ref, zs_ref, q_ref) = rest
    tm = x_ref.shape[1]
    x = x_ref[0]
    h = _rms(x, nin_ref[...]).astype(BF16)

    def proj(i):
        return _dot(h, w_ref[:, i * D_ATT:(i + 1) * D_ATT])

    bones = bones_ref[...]

    def headnorm(t, g_ref):
        tt = t * t
        hi = tt.astype(BF16)
        lo = (tt - hi.astype(F32)).astype(BF16)
        msh = _dot(hi, bones) + _dot(lo, bones)
        return t * lax.rsqrt(msh + RMS_EPS) * g_ref[...]

    qn = headnorm(proj(0), qg_ref)
    kn = headnorm(proj(1), kg_ref)
    v = proj(2)
    k_ref[0] = kn
    v_ref[0] = v
    za_ref[0] = proj(3)
    u_ref[0] = proj(4)
    zs_ref[0] = proj(5)

    f = _dot(h, wf_ref[...]) + bf_ref[...]
    lane = lax.broadcasted_iota(jnp.int32, (tm, LANES), 1)
    lf = jnp.minimum(f, 0.0) - jnp.log1p(jnp.exp(-jnp.abs(f)))
    lf = jnp.where(lane < N_HEADS, lf, 0.0)
    lf_ref[0] = lf

    if not prompt:
        q_ref[0] = qn * (HEAD_DIM ** -0.5)
        return

    @pl.when(pl.program_id(1) == 0)
    def _():
        carry_ref[...] = jnp.zeros_like(carry_ref)

    hi, mid, lo = _split3(lf)
    tri = tri_ref[...]
    c = _dot(tri, hi) + _dot(tri, mid) + _dot(tri, lo) + carry_ref[...]
    carry_ref[...] = c[tm - 1:tm, :]
    c_hi, c_mid, c_lo = _split3(c)
    c_hi, c_mid, c_lo = c_hi.astype(F32), c_mid.astype(F32), c_lo.astype(F32)

    vt = v.T
    row = lax.broadcasted_iota(jnp.int32, (HEAD_DIM, tm), 0)
    ones_row = jnp.where(row == 0, 1.0, 0.0).astype(F32)
    is_one = (lane >= AUG_ONE) & (lane < AUG_ONE + 3)
    for hd in range(N_HEADS):
        pair = hd // 2
        qb = qn[:, pair * LANES:(pair + 1) * LANES]
        kb = kn[:, pair * LANES:(pair + 1) * LANES]
        if hd % 2 == 1:
            qb = pltpu.roll(qb, HEAD_DIM, axis=1)
            kb = pltpu.roll(kb, HEAD_DIM, axis=1)
        chi = c_hi[:, hd:hd + 1]
        cmid = c_mid[:, hd:hd + 1]
        clo = c_lo[:, hd:hd + 1]
        aug = jnp.where(lane == AUG_C, chi, jnp.where(lane == AUG_C + 1, cmid,
                        jnp.where(lane == AUG_C + 2, clo, 0.0)))
        qa = jnp.where(lane < HEAD_DIM, qb * (HEAD_DIM ** -0.5), jnp.where(is_one, 1.0, aug))
        augk = jnp.where(lane == AUG_ONE, -chi, jnp.where(lane == AUG_ONE + 1, -cmid,
                         jnp.where(lane == AUG_ONE + 2, -clo, 0.0)))
        is_onek = (lane >= AUG_C) & (lane < AUG_C + 3)
        ka = jnp.where(lane < HEAD_DIM, kb, jnp.where(is_onek, 1.0, augk))
        qa_ref[0, hd] = qa.astype(BF16)
        ka_ref[0, hd] = ka.astype(BF16)
        vth = vt[hd * HEAD_DIM:(hd + 1) * HEAD_DIM, :]
        vta_ref[0, hd] = jnp.concatenate([vth, ones_row], axis=0).astype(BF16)


def _inproj(x3, prompt, wts, tm):
    nb, rows, dm = x3.shape
    grid = (nb, rows // tm)
    const = lambda shape: pl.BlockSpec(shape, lambda b, j: (0,) * len(shape))
    row_spec = lambda w: pl.BlockSpec((1, tm, w), lambda b, j: (b, j, 0))
    in_specs = [row_spec(dm), const((1, dm)), const((dm, 6 * D_ATT)), const((dm, LANES)),
                const((1, LANES)), const((1, D_ATT)), const((1, D_ATT)), const((D_ATT, D_ATT)),
                const((tm, tm))]
    f32_rows = lambda w: jax.ShapeDtypeStruct((nb, rows, w), F32)
    out_shape = [f32_rows(D_ATT), f32_rows(D_ATT), f32_rows(LANES), f32_rows(D_ATT),
                 f32_rows(D_SSM), f32_rows(D_SSM)]
    out_specs = [row_spec(D_ATT), row_spec(D_ATT), row_spec(LANES), row_spec(D_ATT),
                 row_spec(D_SSM), row_spec(D_SSM)]
    scratch = []
    if prompt:
        out_shape += [jax.ShapeDtypeStruct((nb, N_HEADS, rows, LANES), BF16),
                      jax.ShapeDtypeStruct((nb, N_HEADS, rows, LANES), BF16),
                      jax.ShapeDtypeStruct((nb, N_HEADS, LANES, rows), BF16)]
        out_specs += [pl.BlockSpec((1, N_HEADS, tm, LANES), lambda b, j: (b, 0, j, 0)),
                      pl.BlockSpec((1, N_HEADS, tm, LANES), lambda b, j: (b, 0, j, 0)),
                      pl.BlockSpec((1, N_HEADS, LANES, tm), lambda b, j: (b, 0, 0, j))]
        scratch = [pltpu.VMEM((1, LANES), F32)]
    else:
        out_shape += [f32_rows(D_ATT)]
        out_specs += [row_spec(D_ATT)]
    tri = jnp.tril(jnp.ones((tm, tm), F32)).astype(BF16)
    return pl.pallas_call(
        functools.partial(_inproj_body, prompt=prompt),
        grid=grid, in_specs=in_specs, out_specs=out_specs, out_shape=out_shape,
        scratch_shapes=scratch,
        compiler_params=pltpu.CompilerParams(
            dimension_semantics=("arbitrary", "arbitrary"), vmem_limit_bytes=VMEM_LIMIT),
        name="inproj_prompt" if prompt else "inproj_sample",
    )(x3, wts["norm_in"], wts["w_cat"], wts["w_f"], wts["b_f"], wts["q_gain"], wts["k_gain"],
      wts["bones"], tri)


def _attn_body(qa_ref, ka_ref, vta_ref, o_ref, m_ref, acc_ref, *, tq, tk):
    i = pl.program_id(2)
    q = qa_ref[0, 0]
    m_ref[...] = jnp.full(m_ref.shape, -jnp.inf, F32)
    acc_ref[...] = jnp.zeros(acc_ref.shape, F32)

    def step(j, masked):
        koff = pl.multiple_of(j * tk, tk)
        kc = ka_ref[0, 0, pl.ds(koff, tk), :]
        s = _dot_nt(kc, q)
        if masked:
            kpos = koff + lax.broadcasted_iota(jnp.int32, (tk, tq), 0)
            qpos = i * tq + lax.broadcasted_iota(jnp.int32, (tk, tq), 1)
            s = jnp.where(kpos <= qpos, s, -jnp.inf)
        m_old = m_ref[...]
        m_new = jnp.maximum(m_old, jnp.max(s, axis=0, keepdims=True))
        alpha = jnp.exp(m_old - m_new)
        p = jnp.exp(s - m_new).astype(BF16)
        vc = vta_ref[0, 0, :, pl.ds(koff, tk)]
        acc_ref[...] = alpha * acc_ref[...] + _dot(vc, p)
        m_ref[...] = m_new

    ratio = tq // tk

    def full_step(j, carry):
        step(j, False)
        return carry

    lax.fori_loop(0, i * ratio, full_step, 0)
    for d in range(ratio):
        step(i * ratio + d, True)
    acc = acc_ref[...]
    o_ref[0, 0] = acc[0:HEAD_DIM, :] / acc[HEAD_DIM:HEAD_DIM + 1, :]


def _prompt_attention(qa, ka, vta, tq, tk):
    nb, nh, s, _ = qa.shape
    return pl.pallas_call(
        functools.partial(_attn_body, tq=tq, tk=tk),
        grid=(nb, nh, s // tq),
        in_specs=[pl.BlockSpec((1, 1, tq, LANES), lambda b, h, i: (b, h, i, 0)),
                  pl.BlockSpec((1, 1, s, LANES), lambda b, h, i: (b, h, 0, 0)),
                  pl.BlockSpec((1, 1, LANES, s), lambda b, h, i: (b, h, 0, 0))],
        out_specs=pl.BlockSpec((1, 1, HEAD_DIM, tq), lambda b, h, i: (b, h, 0, i)),
        out_shape=jax.ShapeDtypeStruct((nb, nh, HEAD_DIM, s), F32),
        scratch_shapes=[pltpu.VMEM((1, tq), F32), pltpu.VMEM((LANES, tq), F32)],
        compiler_params=pltpu.CompilerParams(
            dimension_semantics=("arbitrary", "arbitrary", "arbitrary"),
            vmem_limit_bytes=VMEM_LIMIT),
        name="prompt_attn",
    )(qa, ka, vta)


def _swap_halves(x):
    return jnp.concatenate([x[:, HALF_STATE:], x[:, :HALF_STATE]], axis=1)


def _cmul(x, ta, tb):
    return x * ta + _swap_halves(x) * tb


def _ssm_post(y_lin, u, zs, dvec, wglu, bglu, gnorm):
    y = y_lin + dvec * u
    g = _gelu_tanh(y)
    y2 = g * _sigmoid(_dot(g.astype(BF16), wglu) + bglu)
    return _rms(y2 * _silu(zs), gnorm)


def _ssm_prompt_body(u_ref, zs_ref, bfull_ref, cfull_ref, tna_ref, tnb_ref, tpa_ref, tpb_ref,
                     hca_ref, hcb_ref, tri_ref, d_ref, wglu_ref, bglu_ref, gn_ref,
                     ys_ref, hlast_ref, hin_ref):
    @pl.when(pl.program_id(1) == 0)
    def _():
        hin_ref[...] = jnp.zeros_like(hin_ref)

    u = u_ref[0]
    ell = u.shape[0]
    e = _dot(u.astype(BF16), bfull_ref[...])
    z = _cmul(e, tna_ref[...], tnb_ref[...])
    hc = _cmul(hin_ref[...], hca_ref[...], hcb_ref[...])
    cs = _dot(tri_ref[...], z.astype(BF16)) + hc
    hst = _cmul(cs, tpa_ref[...], tpb_ref[...])
    hlast = hst[ell - 1:ell, :]
    hin_ref[...] = hlast
    hlast_ref[0] = hlast
    y_lin = _dot(hst.astype(BF16), cfull_ref[...])
    ys_ref[0] = _ssm_post(y_lin, u, zs_ref[0], d_ref[...], wglu_ref[...], bglu_ref[...],
                          gn_ref[...])


def _ssm_prompt(u, zs, sw, ell):
    nb, s, _ = u.shape
    const = lambda shape: pl.BlockSpec(shape, lambda b, j: (0,) * len(shape))
    row_spec = pl.BlockSpec((1, ell, D_SSM), lambda b, j: (b, j, 0))
    tri = jnp.tril(jnp.ones((ell, ell), F32)).astype(BF16)
    return pl.pallas_call(
        _ssm_prompt_body,
        grid=(nb, s // ell),
        in_specs=[row_spec, row_spec, const((D_SSM, STATE_W)), const((STATE_W, D_SSM)),
                  const((ell, STATE_W)), const((ell, STATE_W)), const((ell, STATE_W)),
                  const((ell, STATE_W)), const((1, STATE_W)), const((1, STATE_W)),
                  const((ell, ell)), const((1, D_SSM)), const((D_SSM, D_SSM)), const((1, D_SSM)),
                  const((1, D_SSM))],
        out_specs=[row_spec, pl.BlockSpec((1, 1, STATE_W), lambda b, j: (b, 0, 0))],
        out_shape=[jax.ShapeDtypeStruct((nb, s, D_SSM), F32),
                   jax.ShapeDtypeStruct((nb, 1, STATE_W), F32)],
        scratch_shapes=[pltpu.VMEM((1, STATE_W), F32)],
        compiler_params=pltpu.CompilerParams(
            dimension_semantics=("arbitrary", "arbitrary"), vmem_limit_bytes=VMEM_LIMIT),
        name="ssm_prompt",
    )(u, zs, sw["bfull"], sw["cfull"], sw["tna"], sw["tnb"], sw["tpa"], sw["tpb"],
      sw["hca"], sw["hcb"], tri, sw["d"], sw["wglu"], sw["bglu"], sw["gnorm"])


def _ssm_sample_body(u_ref, zs_ref, h0_ref, bfull_ref, cfull_ref, la_ref, lb_ref, d_ref,
                     wglu_ref, bglu_ref, gn_ref, ys_ref, hout_ref, *, n_new):
    h = h0_ref[...]
    ys_ref[...] = jnp.zeros_like(ys_ref)
    for t in range(n_new):
        u = u_ref[:, t, :]
        h = _cmul(h, la_ref[...], lb_ref[...]) + _dot(u.astype(BF16), bfull_ref[...])
        y_lin = _dot(h.astype(BF16), cfull_ref[...])
        ys_ref[:, t, :] = _ssm_post(y_lin, u, zs_ref[:, t, :], d_ref[...], wglu_ref[...],
                                    bglu_ref[...], gn_ref[...])
    hout_ref[...] = h


def _ssm_sample(u, zs, h0, sw, n_new):
    nseq = u.shape[0]
    return pl.pallas_call(
        functools.partial(_ssm_sample_body, n_new=n_new),
        out_shape=[jax.ShapeDtypeStruct(u.shape, F32),
                   jax.ShapeDtypeStruct((nseq, STATE_W), F32)],
        compiler_params=pltpu.CompilerParams(vmem_limit_bytes=VMEM_LIMIT),
        name="ssm_sample",
    )(u, zs, h0, sw["bfull"], sw["cfull"], sw["la"], sw["lb"], sw["d"], sw["wglu"], sw["bglu"],
      sw["gnorm"])


def _decode_body(pt_ref, q_ref, kn_ref, vn_ref, lfn_ref, sut_ref, *rest, n_pages, n_new):
    k_refs = rest[0:n_pages]
    v_refs = rest[n_pages:2 * n_pages]
    lf_refs = rest[2 * n_pages:3 * n_pages]
    o_ref = rest[3 * n_pages]
    qblk_ref, m_ref, l_ref, acc_ref, carry_ref = rest[3 * n_pages + 1:]
    g = pl.program_id(1)
    rows = DEC_ROWS * N_HEADS
    page = k_refs[0].shape[1]
    sut = sut_ref[...]

    def update(blocks, mask):
        nblk = len(blocks)
        lf_all = jnp.concatenate([b[2] for b in blocks], axis=0)
        hi, mid, lo = _split3(lf_all)
        sfx = _dot(hi, sut) + _dot(mid, sut) + _dot(lo, sut)
        tots = jnp.sum(lf_all, axis=1, keepdims=True)
        c = carry_ref[...]
        qblk = qblk_ref[...]
        s_list = [None] * nblk
        for bi in reversed(range(nblk)):
            bias = sfx[bi * N_HEADS:(bi + 1) * N_HEADS, :] + c
            c = c + tots[bi * N_HEADS:(bi + 1) * N_HEADS, :]
            s = _dot_nt(qblk, blocks[bi][0].astype(BF16))
            s = s + jnp.concatenate([bias] * DEC_ROWS, axis=0)
            if mask:
                qi = lax.broadcasted_iota(jnp.int32, (rows, page), 0) // N_HEADS
                tok = lax.broadcasted_iota(jnp.int32, (rows, page), 1)
                s = jnp.where((tok <= qi) & (tok < n_new), s, -jnp.inf)
            s_list[bi] = s
        carry_ref[...] = c
        smax = s_list[0]
        for s in s_list[1:]:
            smax = jnp.maximum(smax, s)
        m_old = m_ref[...]
        m_new = jnp.maximum(m_old, jnp.max(smax, axis=1, keepdims=True))
        alpha = jnp.exp(m_old - m_new)
        psum = None
        pv = None
        for bi in range(nblk):
            p = jnp.exp(s_list[bi] - m_new)
            psum = p if psum is None else psum + p
            d = _dot(p.astype(BF16), blocks[bi][1].astype(BF16))
            pv = d if pv is None else pv + d
        l_ref[...] = alpha * l_ref[...] + jnp.sum(psum, axis=1, keepdims=True)
        acc_ref[...] = alpha * acc_ref[...] + pv
        m_ref[...] = m_new

    @pl.when(g == 0)
    def _():
        q = q_ref[0]
        col_head = lax.broadcasted_iota(jnp.int32, (N_HEADS, D_ATT), 1) // HEAD_DIM
        row_head = lax.broadcasted_iota(jnp.int32, (N_HEADS, D_ATT), 0)
        pieces = [jnp.where(col_head == row_head, q[t:t + 1, :], 0.0) for t in range(DEC_ROWS)]
        qblk_ref[...] = jnp.concatenate(pieces, axis=0).astype(BF16)
        m_ref[...] = jnp.full(m_ref.shape, -jnp.inf, F32)
        l_ref[...] = jnp.zeros(l_ref.shape, F32)
        acc_ref[...] = jnp.zeros(acc_ref.shape, F32)
        carry_ref[...] = jnp.zeros(carry_ref.shape, F32)
        pad = jnp.zeros((page - DEC_ROWS, D_ATT), F32)
        kn = jnp.concatenate([kn_ref[0], pad], axis=0)
        vn = jnp.concatenate([vn_ref[0], pad], axis=0)
        r8 = lax.broadcasted_iota(jnp.int32, (DEC_ROWS, LANES), 0)
        lfn = jnp.where(r8 < n_new, lfn_ref[0], 0.0)
        lfn = jnp.concatenate([lfn, jnp.zeros((LANES - DEC_ROWS, LANES), F32)], axis=0)
        lfn_t = lfn.T[0:N_HEADS, :]
        update([(kn, vn, lfn_t)], True)

    update([(k_refs[i][0], v_refs[i][0], lf_refs[i][0]) for i in range(n_pages)], False)

    @pl.when(g == pl.num_programs(1) - 1)
    def _():
        o = acc_ref[...] / l_ref[...]
        col_head = lax.broadcasted_iota(jnp.int32, (rows, D_ATT), 1) // HEAD_DIM
        row_head = lax.broadcasted_iota(jnp.int32, (rows, D_ATT), 0) % N_HEADS
        o = jnp.where(col_head == row_head, o, 0.0)
        o_ref[0] = jnp.sum(o.reshape(DEC_ROWS, N_HEADS, D_ATT), axis=1)


def _decode_attention(page_table, q, kn, vn, lfn, cache_k, cache_v, cache_lft, n_new, n_pages):
    nseq, n_log = page_table.shape
    page = cache_k.shape[1]
    assert page == LANES
    n_groups = n_log // n_pages
    pt_flat = page_table.reshape(-1)
    sut = (jnp.arange(page)[:, None] > jnp.arange(page)[None, :]).astype(BF16)

    def page_map(i):
        return lambda b, g, pt: (pt[b * n_log + (n_groups - 1 - g) * n_pages + i], 0, 0)

    seq_spec = lambda w: pl.BlockSpec((1, DEC_ROWS, w), lambda b, g, pt: (b, 0, 0))
    in_specs = [seq_spec(D_ATT), seq_spec(D_ATT), seq_spec(D_ATT), seq_spec(LANES),
                pl.BlockSpec((page, page), lambda b, g, pt: (0, 0))]
    in_specs += [pl.BlockSpec((1, page, D_ATT), page_map(i)) for i in range(n_pages)]
    in_specs += [pl.BlockSpec((1, page, D_ATT), page_map(i)) for i in range(n_pages)]
    in_specs += [pl.BlockSpec((1, N_HEADS, page), page_map(i)) for i in range(n_pages)]
    rows = DEC_ROWS * N_HEADS
    grid_spec = pltpu.PrefetchScalarGridSpec(
        num_scalar_prefetch=1, grid=(nseq, n_groups), in_specs=in_specs,
        out_specs=pl.BlockSpec((1, DEC_ROWS, D_ATT), lambda b, g, pt: (b, 0, 0)),
        scratch_shapes=[pltpu.VMEM((rows, D_ATT), BF16), pltpu.VMEM((rows, 1), F32),
                        pltpu.VMEM((rows, 1), F32), pltpu.VMEM((rows, D_ATT), F32),
                        pltpu.VMEM((N_HEADS, 1), F32)])
    return pl.pallas_call(
        functools.partial(_decode_body, n_pages=n_pages, n_new=n_new),
        grid_spec=grid_spec,
        out_shape=jax.ShapeDtypeStruct((nseq, DEC_ROWS, D_ATT), F32),
        compiler_params=pltpu.CompilerParams(
            dimension_semantics=("arbitrary", "arbitrary"), vmem_limit_bytes=VMEM_LIMIT),
        name="decode_attn",
    )(pt_flat, q, kn, vn, lfn, sut, *([cache_k] * n_pages), *([cache_v] * n_pages),
      *([cache_lft] * n_pages))


def _merge_body(x_ref, o_ref, za_ref, ys_ref, p_ref, gatt_ref, wo_a_ref, wo_s_ref, wple_ref,
                wgate_ref, y_ref, *, o_transposed):
    o = o_ref[0].T if o_transposed else o_ref[0]
    a = _rms(o * _silu(za_ref[0]), gatt_ref[...])
    h = x_ref[0] + _dot(a.astype(BF16), wo_a_ref[...]) + _dot(ys_ref[0].astype(BF16), wo_s_ref[...])
    gate = _sigmoid(_dot(h.astype(BF16), wgate_ref[...]))
    y_ref[0] = h + gate * _dot(p_ref[0].astype(BF16), wple_ref[...])


def _merge(x3, o, za, ys, p3, wts, tm, o_transposed):
    nb, rows, dm = x3.shape
    ple = p3.shape[-1]
    const = lambda shape: pl.BlockSpec(shape, lambda b, j: (0,) * len(shape))
    row_spec = lambda w: pl.BlockSpec((1, tm, w), lambda b, j: (b, j, 0))
    o_spec = (pl.BlockSpec((1, D_ATT, tm), lambda b, j: (b, 0, j)) if o_transposed
              else row_spec(D_ATT))
    return pl.pallas_call(
        functools.partial(_merge_body, o_transposed=o_transposed),
        grid=(nb, rows // tm),
        in_specs=[row_spec(dm), o_spec, row_spec(D_ATT), row_spec(D_SSM), row_spec(ple),
                  const((1, D_ATT)), const((D_ATT, dm)), const((D_SSM, dm)), const((ple, dm)),
                  const((dm, dm))],
        out_specs=row_spec(dm),
        out_shape=jax.ShapeDtypeStruct((nb, rows, dm), F32),
        compiler_params=pltpu.CompilerParams(
            dimension_semantics=("arbitrary", "arbitrary"), vmem_limit_bytes=VMEM_LIMIT),
        name="merge_prompt" if o_transposed else "merge_sample",
    )(x3, o, za, ys, p3, wts["g_att"], wts["wo_a"], wts["wo_s"], wts["w_ple"], wts["w_gate"])


def _layer_weights(norm_in, w_in, b_forget, q_norm, k_norm, att_out_norm, w_out, w_ple, w_ple_gate):
    dm = w_in.shape[0]
    c0, c1, c2 = D_ATT, 2 * D_ATT, 3 * D_ATT
    c3 = c2 + N_HEADS
    w_cat = jnp.concatenate([w_in[:, :c2], w_in[:, c3:]], axis=1).astype(BF16)
    w_f = jnp.zeros((dm, LANES), F32).at[:, :N_HEADS].set(w_in[:, c2:c3]).astype(BF16)
    b_f = jnp.zeros((1, LANES), F32).at[0, :N_HEADS].set(b_forget)
    head = jnp.arange(D_ATT) // HEAD_DIM
    bones = ((head[:, None] == head[None, :]).astype(F32) / HEAD_DIM).astype(BF16)
    return dict(
        norm_in=norm_in[None, :], w_cat=w_cat, w_f=w_f, b_f=b_f,
        q_gain=jnp.tile(q_norm, N_HEADS)[None, :], k_gain=jnp.tile(k_norm, N_HEADS)[None, :],
        bones=bones, g_att=att_out_norm[None, :],
        wo_a=w_out[:D_ATT].astype(BF16), wo_s=w_out[D_ATT:].astype(BF16),
        w_ple=w_ple.astype(BF16), w_gate=w_ple_gate.astype(BF16))


def _ssm_weights(a_re, a_im, log_dt, b_re, b_im, c_re, c_im, d, w_glu, b_glu, ssm_out_norm, ell):
    dt = jnp.exp(log_dt)[:, None]
    zr, zi = a_re * dt, a_im * dt

    def lam_pow(e):
        e = jnp.asarray(e, F32)[:, None, None]
        mag = jnp.exp(zr[None] * e)
        re = (mag * jnp.cos(zi[None] * e)).reshape(-1, HALF_STATE)
        im = (mag * jnp.sin(zi[None] * e)).reshape(-1, HALF_STATE)
        return jnp.concatenate([re, re], axis=1), jnp.concatenate([-im, im], axis=1)

    nr = jnp.expm1(zr) * jnp.cos(zi) - 2.0 * jnp.sin(0.5 * zi) ** 2
    ni = jnp.exp(zr) * jnp.sin(zi)
    den = a_re * a_re + a_im * a_im
    fr = (nr * a_re + ni * a_im) / den
    fi = (ni * a_re - nr * a_im) / den
    bb_re = fr[..., None] * b_re - fi[..., None] * b_im
    bb_im = fr[..., None] * b_im + fi[..., None] * b_re
    eye = jnp.eye(N_GROUPS, dtype=F32)
    to_in = lambda m: jnp.einsum("gpc,gh->gchp", m, eye).reshape(D_SSM, HALF_STATE)
    bfull = jnp.concatenate([to_in(bb_re), to_in(bb_im)], axis=1).astype(BF16)
    to_out = lambda m: jnp.einsum("gcp,gh->gphc", m, eye).reshape(HALF_STATE, D_SSM)
    cfull = jnp.concatenate([to_out(c_re), to_out(-c_im)], axis=0).astype(BF16)
    c0 = ell // 2
    steps = jnp.arange(ell)
    tna, tnb = lam_pow(c0 - steps)
    tpa, tpb = lam_pow(steps - c0)
    hca, hcb = lam_pow(jnp.array([c0 + 1]))
    la, lb = lam_pow(jnp.array([1]))
    return dict(bfull=bfull, cfull=cfull, tna=tna, tnb=tnb, tpa=tpa, tpb=tpb, hca=hca, hcb=hcb,
                la=la, lb=lb, d=d.reshape(1, D_SSM), wglu=w_glu.astype(BF16),
                bglu=b_glu[None, :], gnorm=ssm_out_norm[None, :])


def _state_out(h):
    n = h.shape[0]
    return (h[:, :HALF_STATE].reshape(n, N_GROUPS, SSM_STATE),
            h[:, HALF_STATE:].reshape(n, N_GROUPS, SSM_STATE))


def _pick_tile(n, target):
    t = min(n, target)
    while n % t:
        t //= 2
    return t


def kernel(x_prompt, x_sample, p_prompt, p_sample, cache_k, cache_v, cache_logf, state_ssm_re, state_ssm_im, page_table, norm_in, w_in, b_forget, q_norm, k_norm, ssm_a_re, ssm_a_im, ssm_log_dt, ssm_b_re, ssm_b_im, ssm_c_re, ssm_c_im, ssm_d, w_glu, b_glu, att_out_norm, ssm_out_norm, w_out, w_ple, w_ple_gate):
    depth = norm_in.shape[0]
    nb, seq, dm = x_prompt.shape
    nseq, n_new, _ = x_sample.shape
    n_pool, page = cache_k.shape[1], cache_k.shape[2]
    assert n_new <= DEC_ROWS
    ell = _pick_tile(seq, 128)
    tm_in = _pick_tile(seq, 256)
    tm_merge = _pick_tile(seq, 512)
    tq = _pick_tile(seq, 512)
    n_pages = _pick_tile(page_table.shape[1], 16)

    xp = x_prompt
    xs = jnp.pad(x_sample, ((0, 0), (0, DEC_ROWS - n_new), (0, 0))).reshape(1, nseq * DEC_ROWS, dm)
    outs = {k: [] for k in ("kp", "vp", "fp", "ks", "vs", "fs", "srp", "sip", "srs", "sis")}
    for i in range(depth):
        wts = _layer_weights(norm_in[i], w_in[i], b_forget[i], q_norm[i], k_norm[i],
                             att_out_norm[i], w_out[i], w_ple[i], w_ple_gate[i])
        sw = _ssm_weights(ssm_a_re[i], ssm_a_im[i], ssm_log_dt[i], ssm_b_re[i], ssm_b_im[i],
                          ssm_c_re[i], ssm_c_im[i], ssm_d[i], w_glu[i], b_glu[i],
                          ssm_out_norm[i], ell)

        k, v, lf, za, u, zs, qa, ka, vta = _inproj(xp, True, wts, tm_in)
        o_t = _prompt_attention(qa, ka, vta, tq, tq).reshape(nb, D_ATT, seq)
        ys, hlast = _ssm_prompt(u, zs, sw, ell)
        xp = _merge(xp, o_t, za, ys, p_prompt[i], wts, tm_merge, True)
        outs["kp"].append(k.reshape(nb, seq, N_HEADS, HEAD_DIM))
        outs["vp"].append(v.reshape(nb, seq, N_HEADS, HEAD_DIM))
        outs["fp"].append(lf[:, :, :N_HEADS])
        h_re, h_im = _state_out(hlast[:, 0, :])
        outs["srp"].append(h_re)
        outs["sip"].append(h_im)

        rows_s = nseq * DEC_ROWS
        k, v, lf, za, u, zs, q = _inproj(xs, False, wts, _pick_tile(rows_s, 256))
        per_seq = lambda t: t.reshape(nseq, DEC_ROWS, t.shape[-1])
        cache_lft = jnp.swapaxes(cache_logf[i], 1, 2)
        o_s = _decode_attention(page_table, per_seq(q), per_seq(k), per_seq(v), per_seq(lf),
                                cache_k[i].reshape(n_pool, page, D_ATT),
                                cache_v[i].reshape(n_pool, page, D_ATT), cache_lft,
                                n_new, n_pages)
        h0 = jnp.concatenate([state_ssm_re[i].reshape(nseq, HALF_STATE),
                              state_ssm_im[i].reshape(nseq, HALF_STATE)], axis=1)
        ys_s, h_s = _ssm_sample(per_seq(u), per_seq(zs), h0, sw, n_new)
        p_s = jnp.pad(p_sample[i], ((0, 0), (0, DEC_ROWS - n_new), (0, 0))).reshape(1, rows_s, -1)
        xs = _merge(xs, o_s.reshape(1, rows_s, D_ATT), za, ys_s.reshape(1, rows_s, D_SSM), p_s,
                    wts, _pick_tile(rows_s, 512), False)
        outs["ks"].append(per_seq(k)[:, :n_new].reshape(nseq, n_new, N_HEADS, HEAD_DIM))
        outs["vs"].append(per_seq(v)[:, :n_new].reshape(nseq, n_new, N_HEADS, HEAD_DIM))
        outs["fs"].append(per_seq(lf)[:, :n_new, :N_HEADS])
        h_re, h_im = _state_out(h_s)
        outs["srs"].append(h_re)
        outs["sis"].append(h_im)

    st = lambda key: jnp.stack(outs[key])
    y_sample = xs.reshape(nseq, DEC_ROWS, dm)[:, :n_new]
    return (xp, y_sample, st("kp"), st("vp"), st("fp"), st("ks"), st("vs"), st("fs"),
            st("srp"), st("sip"), st("srs"), st("sis"))
```
